```python
import math
import jax, jax.numpy as jnp
from jax import lax
import numpy as np

D_MODEL = 2048
BATCH = 2
SEQ = 4096
DEPTH = 1

MIX_WIDTH = D_MODEL
POOL_WIDTH = MIX_WIDTH // 2
ATTN_WIDTH = MIX_WIDTH - POOL_WIDTH
POOL_WINDOWS = (2, 4, 8, 16)
N_POOL_GROUPS = len(POOL_WINDOWS)
POOL_GROUP = POOL_WIDTH // N_POOL_GROUPS
DIFF_HEAD_DIM = 64
DIFF_V_DIM = 2 * DIFF_HEAD_DIM
N_DIFF_HEADS = ATTN_WIDTH // DIFF_V_DIM
Q_COLS = N_DIFF_HEADS * 2 * DIFF_HEAD_DIM
K_COLS = N_DIFF_HEADS * 2 * DIFF_HEAD_DIM
V_COLS = N_DIFF_HEADS * DIFF_V_DIM
IN_COLS = POOL_WIDTH + Q_COLS + K_COLS + V_COLS
Q_BLOCK = 128
ROPE_THETA = 10000.0
N_EXPERTS = 16
EC_CAPACITY_FACTOR = 2
D_FF = 128 * ((8 * D_MODEL // 3 + 127) // 128)
LN_EPS = 1e-5
RMS_EPS = 1e-5
DEEPNORM_ALPHA = (2.0 * DEPTH) ** 0.25
DEEPNORM_BETA = (8.0 * DEPTH) ** -0.25

kernel_name = "hybrid_pool_diffattn_ec_moe_encoder"


def lambda_init(layer_idx):
    return 0.8 - 0.6 * math.exp(-0.3 * layer_idx)


def layer_norm(x, g, b):
    xf = x.astype(jnp.float32)
    mu = jnp.mean(xf, axis=-1, keepdims=True)
    var = jnp.mean(jnp.square(xf - mu), axis=-1, keepdims=True)
    y = (xf - mu) * lax.rsqrt(var + LN_EPS) * g.astype(jnp.float32) + b.astype(jnp.float32)
    return y.astype(x.dtype)


def rms_norm(x, g):
    xf = x.astype(jnp.float32)
    y = xf * lax.rsqrt(jnp.mean(jnp.square(xf), axis=-1, keepdims=True) + RMS_EPS)
    return (y * g.astype(jnp.float32)).astype(x.dtype)


def rope_tables(seq, dim, dtype):
    pos = jnp.arange(seq, dtype=jnp.float32)
    inv_freq = ROPE_THETA ** (-jnp.arange(0, dim, 2, dtype=jnp.float32) / dim)
    ang = pos[:, None] * inv_freq[None, :]
    ang = jnp.concatenate([ang, ang], axis=-1)
    return jnp.cos(ang).astype(dtype), jnp.sin(ang).astype(dtype)


def apply_rope(t, cos, sin):
    half = t.shape[-1] // 2
    rot = jnp.concatenate([-t[..., half:], t[..., :half]], axis=-1)
    c = cos[None, :, None, None, :]
    s = sin[None, :, None, None, :]
    return t * c + rot * s


def centred_mean_minus_self(u, window):
    B, S, C = u.shape
    uf = u.astype(jnp.float32)
    cs = jnp.concatenate([jnp.zeros((B, 1, C), jnp.float32), jnp.cumsum(uf, axis=1)], axis=1)
    t = jnp.arange(S)
    lo = jnp.clip(t - window // 2, 0, S)
    hi = jnp.clip(t + window - window // 2, 0, S)
    total = cs[:, hi] - cs[:, lo]
    count = (hi - lo).astype(jnp.float32)[None, :, None]
    return (total / count - uf).astype(u.dtype)


def multiscale_pool_mixer(u, pool_w, pool_scale):
    B, S, _ = u.shape
    ug = u.reshape(B, S, N_POOL_GROUPS, POOL_GROUP)
    pooled = jnp.stack(
        [centred_mean_minus_self(ug[:, :, gi], w) for gi, w in enumerate(POOL_WINDOWS)], axis=2)
    mixed = jnp.einsum('bsgc,gcd->bsgd', pooled, pool_w)
    return mixed.reshape(B, S, POOL_WIDTH) * pool_scale


def diff_attention(q, k, v, cos, sin, lq1, lk1, lq2, lk2, subln_g, lam_init):
    B, S, _ = q.shape
    H, d = N_DIFF_HEADS, DIFF_HEAD_DIM
    q = apply_rope(q.reshape(B, S, H, 2, d), cos, sin) * (d ** -0.5)
    k = apply_rope(k.reshape(B, S, H, 2, d), cos, sin)
    v = v.reshape(B, S, H, DIFF_V_DIM).transpose(0, 2, 1, 3)
    q = q.transpose(0, 2, 3, 1, 4)
    k = k.transpose(0, 2, 3, 1, 4)
    lam = (jnp.exp(jnp.sum(lq1.astype(jnp.float32) * lk1.astype(jnp.float32)))
           - jnp.exp(jnp.sum(lq2.astype(jnp.float32) * lk2.astype(jnp.float32)))
           + lam_init)
    n_blocks = S // Q_BLOCK
    qb = q.reshape(B, H, 2, n_blocks, Q_BLOCK, d).transpose(3, 0, 1, 2, 4, 5)

    def block(q_blk):
        scores = jnp.einsum('bhcqd,bhckd->bhcqk', q_blk, k).astype(jnp.float32)
        probs = jax.nn.softmax(scores, axis=-1)
        diff = (probs[:, :, 0] - lam * probs[:, :, 1]).astype(v.dtype)
        return jnp.einsum('bhqk,bhkv->bhqv', diff, v)

    out = lax.map(block, qb)
    out = out.transpose(1, 0, 3, 2, 4).reshape(B, S, H, DIFF_V_DIM)
    out = rms_norm(out, subln_g) * (1.0 - lam_init)
    return out.reshape(B, S, ATTN_WIDTH)


def expert_choice_ffn(x, w_router, w_gate, w_up, w_down):
    B, S, D = x.shape
    cap = EC_CAPACITY_FACTOR * S // N_EXPERTS
    logits = jnp.einsum('bsd,de->bse', x, w_router).astype(jnp.float32)
    affinity = jax.nn.softmax(logits, axis=-1)
    gate, idx = lax.top_k(affinity.transpose(0, 2, 1), cap)
    xin = jax.vmap(lambda xb, ib: xb[ib])(x, idx)
    h = jax.nn.silu(jnp.einsum('becd,edf->becf', xin, w_gate)) * jnp.einsum('becd,edf->becf', xin, w_up)
    y = jnp.einsum('becf,efd->becd', h, w_down) * gate[..., None].astype(x.dtype)
    return jax.vmap(lambda yb, ib: jnp.zeros((S, D), yb.dtype).at[ib.reshape(-1)].add(yb.reshape(-1, D)))(y, idx)


def setup_inputs(seed: int = 0) -> dict:
    key = jax.random.key(seed)
    ks = jax.random.split(key, 20)
    f32 = jnp.float32
    nrm = lambda k, shape, s: jax.random.normal(k, shape, f32) * s
    x = jax.random.normal(ks[0], (BATCH, SEQ, D_MODEL), f32)
    w_in = nrm(ks[1], (DEPTH, D_MODEL, IN_COLS), D_MODEL ** -0.5)
    v_scale = jnp.concatenate([jnp.ones((IN_COLS - V_COLS,), f32), jnp.full((V_COLS,), DEEPNORM_BETA, f32)])
    w_in = w_in * v_scale
    pool_w = nrm(ks[2], (DEPTH, N_POOL_GROUPS, POOL_GROUP, POOL_GROUP), POOL_GROUP ** -0.5)
    pool_scale = 1.0 + nrm(ks[3], (DEPTH, POOL_WIDTH), 0.1)
    lambda_q1 = nrm(ks[4], (DEPTH, DIFF_HEAD_DIM), 0.1)
    lambda_k1 = nrm(ks[5], (DEPTH, DIFF_HEAD_DIM), 0.1)
    lambda_q2 = nrm(ks[6], (DEPTH, DIFF_HEAD_DIM), 0.1)
    lambda_k2 = nrm(ks[7], (DEPTH, DIFF_HEAD_DIM), 0.1)
    subln_g = 1.0 + nrm(ks[8], (DEPTH, DIFF_V_DIM), 0.02)
    w_out = nrm(ks[9], (DEPTH, MIX_WIDTH, D_MODEL), MIX_WIDTH ** -0.5 * DEEPNORM_BETA)
    ln1_g = 1.0 + nrm(ks[10], (DEPTH, D_MODEL), 0.02)
    ln1_b = nrm(ks[11], (DEPTH, D_MODEL), 0.02)
    w_router = nrm(ks[12], (DEPTH, D_MODEL, N_EXPERTS), D_MODEL ** -0.5)
    w_gate = nrm(ks[13], (DEPTH, N_EXPERTS, D_MODEL, D_FF), D_MODEL ** -0.5)
    w_up = nrm(ks[14], (DEPTH, N_EXPERTS, D_MODEL, D_FF), D_MODEL ** -0.5)
    w_down = nrm(ks[15], (DEPTH, N_EXPERTS, D_FF, D_MODEL), D_FF ** -0.5 * DEEPNORM_BETA)
    ln2_g = 1.0 + nrm(ks[16], (DEPTH, D_MODEL), 0.02)
    ln2_b = nrm(ks[17], (DEPTH, D_MODEL), 0.02)
    return {"x": x, "w_in": w_in, "pool_w": pool_w, "pool_scale": pool_scale,
            "lambda_q1": lambda_q1, "lambda_k1": lambda_k1, "lambda_q2": lambda_q2, "lambda_k2": lambda_k2,
            "subln_g": subln_g, "w_out": w_out, "ln1_g": ln1_g, "ln1_b": ln1_b,
            "w_router": w_router, "w_gate": w_gate, "w_up": w_up, "w_down": w_down,
            "ln2_g": ln2_g, "ln2_b": ln2_b}


def reference(x, w_in, pool_w, pool_scale, lambda_q1, lambda_k1, lambda_q2, lambda_k2,
              subln_g, w_out, ln1_g, ln1_b, w_router, w_gate, w_up, w_down, ln2_g, ln2_b):
    B, S, _ = x.shape
    cos, sin = rope_tables(S, DIFF_HEAD_DIM, x.dtype)
    q0 = POOL_WIDTH
    k0 = q0 + Q_COLS
    v0 = k0 + K_COLS
    for l in range(DEPTH):
        u = jnp.einsum('bsd,dc->bsc', x, w_in[l])
        pool_out = multiscale_pool_mixer(u[..., :q0], pool_w[l], pool_scale[l])
        attn_out = diff_attention(u[..., q0:k0], u[..., k0:v0], u[..., v0:], cos, sin,
                                  lambda_q1[l], lambda_k1[l], lambda_q2[l], lambda_k2[l],
                                  subln_g[l], lambda_init(l))
        mixed = jnp.concatenate([pool_out, attn_out], axis=-1)
        mix = jnp.einsum('bsc,cd->bsd', mixed, w_out[l])
        x = layer_norm(DEEPNORM_ALPHA * x + mix, ln1_g[l], ln1_b[l])
        moe = expert_choice_ffn(x, w_router[l], w_gate[l], w_up[l], w_down[l])
        x = layer_norm(DEEPNORM_ALPHA * x + moe, ln2_g[l], ln2_b[l])
    return x
```

```python
import functools
import math

import jax
import jax.numpy as jnp
from jax import lax
from jax.experimental import pallas as pl
from jax.experimental.pallas import tpu as pltpu

F32 = jnp.float32
BF16 = jnp.bfloat16

POOL_WINDOWS = (2, 4, 8, 16)
HEAD_DIM = 64
V_DIM = 2 * HEAD_DIM
ROPE_THETA = 10000.0
N_EXPERTS = 16
CAPACITY_FACTOR = 2
LN_EPS = 1e-5
RMS_EPS = 1e-5
DEPTH = 1
DEEPNORM_ALPHA = (2.0 * DEPTH) ** 0.25
LAMBDA_INIT = 0.8 - 0.6 * math.exp(-0.3 * 0)
Q_SCALE = HEAD_DIM ** -0.5 * math.log2(math.e)

LANES = 128
VMEM_LIMIT_BYTES = 56 * 1024 * 1024
POOL_PAD = 16
GATHER_UNROLL = 8


def _params(*semantics):
    return pltpu.CompilerParams(dimension_semantics=semantics, vmem_limit_bytes=VMEM_LIMIT_BYTES)


def _rope_cols(t, cos, sin_lo, sin_hi):
    out = []
    for g in range(t.shape[1] // LANES):
        tg = t[:, g * LANES:(g + 1) * LANES]
        up = pltpu.roll(tg, LANES - HEAD_DIM // 2, axis=1)
        dn = pltpu.roll(tg, HEAD_DIM // 2, axis=1)
        out.append(tg * cos + up * sin_lo + dn * sin_hi)
    return jnp.concatenate(out, axis=1)


def _inproj_kernel(x_ref, w_ref, cos_ref, slo_ref, shi_ref, pool_ref, q_ref, k_ref, v_ref, *, width):
    xb = x_ref[...].astype(BF16)
    cos, slo, shi = cos_ref[...], slo_ref[...], shi_ref[...]
    pool_ref[...] = jnp.dot(xb, w_ref[:, 0:width], preferred_element_type=F32)
    tq = jnp.dot(xb, w_ref[:, width:2 * width], preferred_element_type=F32)
    q_ref[...] = (_rope_cols(tq, cos, slo, shi) * Q_SCALE).astype(BF16)
    tk = jnp.dot(xb, w_ref[:, 2 * width:3 * width], preferred_element_type=F32)
    k_ref[...] = _rope_cols(tk, cos, slo, shi).astype(BF16)
    v_ref[...] = jnp.dot(xb, w_ref[:, 3 * width:4 * width], preferred_element_type=F32).astype(BF16)


def _in_projection(x2, w_in_b, cos, sin_lo, sin_hi, seq, tm):
    rows, d = x2.shape
    width = w_in_b.shape[1] // 4
    nseq = seq // tm
    tab = pl.BlockSpec((tm, LANES), lambda i: (i % nseq, 0))
    out_spec = pl.BlockSpec((tm, width), lambda i: (i, 0))
    return pl.pallas_call(
        functools.partial(_inproj_kernel, width=width),
        grid=(rows // tm,),
        in_specs=[pl.BlockSpec((tm, d), lambda i: (i, 0)),
                  pl.BlockSpec((d, 4 * width), lambda i: (0, 0), pipeline_mode=pl.Buffered(1)),
                  tab, tab, tab],
        out_specs=[out_spec, out_spec, out_spec, out_spec],
        out_shape=[jax.ShapeDtypeStruct((rows, width), F32),
                   jax.ShapeDtypeStruct((rows, width), BF16),
                   jax.ShapeDtypeStruct((rows, width), BF16),
                   jax.ShapeDtypeStruct((rows, width), BF16)],
        compiler_params=_params("arbitrary"),
        name="in_projection",
    )(x2, w_in_b, cos, sin_lo, sin_hi)


def _pool_kernel(u_ref, w_ref, scale_ref, o_ref, pad_ref, *, seq, chunk):
    g = pl.program_id(1)
    cg = u_ref.shape[1]
    zeros = jnp.zeros((POOL_PAD, cg), F32)
    pad_ref[pl.ds(0, POOL_PAD), :] = zeros
    pad_ref[pl.ds(POOL_PAD + seq, POOL_PAD), :] = zeros
    pad_ref[pl.ds(POOL_PAD, seq), :] = u_ref[...]
    wb = w_ref[0].astype(BF16)
    scale = scale_ref[...]

    for gi, window in enumerate(POOL_WINDOWS):
        half = window // 2

        @pl.when(g == gi)
        def _(half=half, window=window):
            for c in range(seq // chunk):
                r0 = c * chunk
                total = pad_ref[pl.ds(POOL_PAD + r0 - half, chunk), :]
                for j in range(1, window):
                    total = total + pad_ref[pl.ds(POOL_PAD + r0 - half + j, chunk), :]
                t = r0 + lax.broadcasted_iota(jnp.int32, (chunk, cg), 0)
                count = jnp.minimum(t + half, seq) - jnp.maximum(t - half, 0)
                pooled = total / count.astype(F32) - pad_ref[pl.ds(POOL_PAD + r0, chunk), :]
                mixed = jnp.dot(pooled.astype(BF16), wb, preferred_element_type=F32) * scale
                o_ref[pl.ds(r0, chunk), :] = mixed.astype(BF16)


def _pool_mixer(u_pool, pool_w, pool_scale, batch, seq):
    rows, width = u_pool.shape
    ng = len(POOL_WINDOWS)
    cg = width // ng
    chunk = min(512, seq)
    return pl.pallas_call(
        functools.partial(_pool_kernel, seq=seq, chunk=chunk),
        grid=(batch, ng),
        in_specs=[pl.BlockSpec((seq, cg), lambda b, g: (b, g)),
                  pl.BlockSpec((1, cg, cg), lambda b, g: (g, 0, 0)),
                  pl.BlockSpec((1, cg), lambda b, g: (0, g))],
        out_specs=pl.BlockSpec((seq, cg), lambda b, g: (b, g)),
        out_shape=jax.ShapeDtypeStruct((rows, width), BF16),
        scratch_shapes=[pltpu.VMEM((seq + 2 * POOL_PAD, cg), F32)],
        compiler_params=_params("arbitrary", "arbitrary"),
        name="pool_mixer",
    )(u_pool, pool_w, pool_scale)


def _attn_kernel(q_ref, k_ref, v_ref, lq1_ref, lk1_ref, lq2_ref, lk2_ref, g_ref, o_ref, vt_ref, *, seq, tk):
    @pl.when(pl.program_id(2) == 0)
    def _():
        vt_ref[...] = v_ref[...].T

    tq = q_ref.shape[0]
    qt = q_ref[...].T
    sub = lax.broadcasted_iota(jnp.int32, qt.shape, 0)
    zero = jnp.zeros_like(qt)
    qts = (jnp.where(sub < HEAD_DIM, qt, zero), jnp.where(sub >= HEAD_DIM, qt, zero))

    def body(j, carry):
        start = pl.multiple_of(j * tk, tk)
        k_blk = k_ref[pl.ds(start, tk), :]
        vt_blk = vt_ref[:, pl.ds(start, tk)]
        scores = [jnp.dot(k_blk, qts[c], preferred_element_type=F32) for c in range(2)]
        new = []
        for c in range(2):
            m_old, l_old, acc_old = carry[c]
            s = scores[c]
            col_max = jnp.max(jnp.max(s.reshape(tk // 8, 8, tq), axis=0), axis=0, keepdims=True)
            m_new = jnp.maximum(m_old, col_max)
            alpha = jnp.exp2(m_old - m_new)
            p = jnp.exp2(s - m_new)
            l = alpha * l_old + jnp.sum(p.reshape(tk // 8, 8, tq), axis=0)
            acc = alpha * acc_old + jnp.dot(vt_blk, p.astype(BF16), preferred_element_type=F32)
            new.append((m_new, l, acc))
        return tuple(new)

    init = tuple((jnp.full((1, tq), -jnp.inf, F32), jnp.zeros((8, tq), F32), jnp.zeros((V_DIM, tq), F32))
                 for _ in range(2))
    (_, l1, a1), (_, l2, a2) = lax.fori_loop(0, seq // tk, body, init)
    lam = (jnp.exp(jnp.sum(lq1_ref[...] * lk1_ref[...], axis=1, keepdims=True))
           - jnp.exp(jnp.sum(lq2_ref[...] * lk2_ref[...], axis=1, keepdims=True))
           + LAMBDA_INIT)
    ot = a1 / jnp.sum(l1, axis=0, keepdims=True) - lam * (a2 / jnp.sum(l2, axis=0, keepdims=True))
    o = ot.T
    y = o * lax.rsqrt(jnp.mean(o * o, axis=1, keepdims=True) + RMS_EPS) * g_ref[...]
    o_ref[...] = (y * (1.0 - LAMBDA_INIT)).astype(BF16)


def _diff_attention(q, k, v, lq1, lk1, lq2, lk2, subln_g, batch, seq, tq, tk):
    rows, width = q.shape
    heads = width // V_DIM
    nq = seq // tq
    vec = pl.BlockSpec((1, HEAD_DIM), lambda b, h, i: (0, 0))
    kv = pl.BlockSpec((seq, V_DIM), lambda b, h, i: (b, h))
    return pl.pallas_call(
        functools.partial(_attn_kernel, seq=seq, tk=tk),
        grid=(batch, heads, nq),
        in_specs=[pl.BlockSpec((tq, V_DIM), lambda b, h, i: (b * nq + i, h)), kv, kv,
                  vec, vec, vec, vec,
                  pl.BlockSpec((1, V_DIM), lambda b, h, i: (0, 0))],
        out_specs=pl.BlockSpec((tq, V_DIM), lambda b, h, i: (b * nq + i, h)),
        out_shape=jax.ShapeDtypeStruct((rows, width), BF16),
        scratch_shapes=[pltpu.VMEM((V_DIM, seq), BF16)],
        compiler_params=_params("arbitrary", "arbitrary", "arbitrary"),
        name="diff_attention",
    )(q, k, v, lq1, lk1, lq2, lk2, subln_g)


def _layer_norm(y, g, b):
    mu = jnp.mean(y, axis=1, keepdims=True)
    yc = y - mu
    var = jnp.mean(yc * yc, axis=1, keepdims=True)
    return yc * lax.rsqrt(var + LN_EPS) * g + b


def _outproj_kernel(pool_ref, attn_ref, x_ref, w_ref, g_ref, b_ref, wr_ref, x1_ref, aff_ref):
    pw = pool_ref.shape[1]
    mix = jnp.dot(pool_ref[...], w_ref[pl.ds(0, pw), :], preferred_element_type=F32)
    mix = mix + jnp.dot(attn_ref[...], w_ref[pl.ds(pw, w_ref.shape[0] - pw), :], preferred_element_type=F32)
    x1 = _layer_norm(DEEPNORM_ALPHA * x_ref[...] + mix, g_ref[...], b_ref[...])
    x1_ref[...] = x1
    logits = lax.dot_general(wr_ref[...], x1, (((1,), (1,)), ((), ())),
                             precision=lax.Precision.HIGHEST, preferred_element_type=F32)
    e = jnp.exp(logits - jnp.max(logits, axis=0, keepdims=True))
    aff_ref[...] = e / jnp.sum(e, axis=0, keepdims=True)


def _out_projection(pool_out, attn_out, x2, w_out_b, ln_g, ln_b, w_router_t, tm):
    rows, d = x2.shape
    pw, aw = pool_out.shape[1], attn_out.shape[1]
    ne = w_router_t.shape[0]
    row = pl.BlockSpec((1, d), lambda i: (0, 0))
    return pl.pallas_call(
        _outproj_kernel,
        grid=(rows // tm,),
        in_specs=[pl.BlockSpec((tm, pw), lambda i: (i, 0)),
                  pl.BlockSpec((tm, aw), lambda i: (i, 0)),
                  pl.BlockSpec((tm, d), lambda i: (i, 0)),
                  pl.BlockSpec((pw + aw, d), lambda i: (0, 0), pipeline_mode=pl.Buffered(1)),
                  row, row,
                  pl.BlockSpec((ne, d), lambda i: (0, 0))],
        out_specs=[pl.BlockSpec((tm, d), lambda i: (i, 0)),
                   pl.BlockSpec((ne, tm), lambda i: (0, i))],
        out_shape=[jax.ShapeDtypeStruct((rows, d), F32),
                   jax.ShapeDtypeStruct((ne, rows), F32)],
        compiler_params=_params("arbitrary"),
        name="out_projection",
    )(pool_out, attn_out, x2, w_out_b, ln_g, ln_b, w_router_t)


def _lane_cumsum(x, n):
    lane = lax.broadcasted_iota(jnp.int32, x.shape, 1)
    k = 1
    while k < n:
        x = x + jnp.where(lane >= k, pltpu.roll(x, k, axis=1), 0.0)
        k *= 2
    return x


def _route_kernel(aff_ref, idx_ref, gate_ref, *, seq, cap, slot_chunk):
    b = pl.program_id(0)
    a = aff_ref[...]
    ne = a.shape[0]

    thr = jnp.zeros((ne, 1), jnp.int32)
    for bit in range(30, -1, -1):
        cand = thr | (1 << bit)
        cnt = jnp.sum((a >= pltpu.bitcast(cand, F32)).astype(F32), axis=1, keepdims=True)
        thr = jnp.where(cnt >= cap, cand, thr)
    gt = a >= pltpu.bitcast(thr + 1, F32)
    eq = jnp.logical_and(a >= pltpu.bitcast(thr, F32), jnp.logical_not(gt))
    need = cap - jnp.sum(gt.astype(F32), axis=1, keepdims=True)
    eq_rank = _lane_cumsum(eq.astype(F32), seq)
    sel = gt | (eq & (eq_rank <= need))
    rank = jnp.where(sel, _lane_cumsum(sel.astype(F32), seq), 0.0)

    tok = lax.broadcasted_iota(jnp.int32, (slot_chunk, seq), 1).astype(F32)
    for e in range(ne):
        rank_e = rank[e:e + 1, :]
        a_e = a[e:e + 1, :]
        for c in range(cap // slot_chunk):
            slot = (c * slot_chunk + 1 + lax.broadcasted_iota(jnp.int32, (slot_chunk, seq), 0)).astype(F32)
            hit = rank_e == slot
            idx = jnp.sum(jnp.where(hit, tok, 0.0), axis=1, keepdims=True)
            gate = jnp.sum(jnp.where(hit, a_e, 0.0), axis=1, keepdims=True)
            idx_ref[0, pl.ds(c * slot_chunk, slot_chunk), e:e + 1] = idx.astype(jnp.int32) + b * seq
            gate_ref[0, pl.ds(c * slot_chunk, slot_chunk), e:e + 1] = gate


def _route(aff_t, batch, seq, cap):
    ne = aff_t.shape[0]
    slot_chunk = min(128, cap)
    out = pl.BlockSpec((1, cap, ne), lambda b: (b, 0, 0))
    return pl.pallas_call(
        functools.partial(_route_kernel, seq=seq, cap=cap, slot_chunk=slot_chunk),
        grid=(batch,),
        in_specs=[pl.BlockSpec((ne, seq), lambda b: (0, b))],
        out_specs=[out, out],
        out_shape=[jax.ShapeDtypeStruct((batch, cap, ne), jnp.int32),
                   jax.ShapeDtypeStruct((batch, cap, ne), F32)],
        compiler_params=_params("arbitrary"),
        name="route",
    )(aff_t)


def _gather_kernel(idx_ref, x_hbm, o_ref, buf_ref, sem, *, rows_per_step):
    step = pl.program_id(0) * pl.num_programs(1) + pl.program_id(1)
    n_steps = pl.num_programs(0) * pl.num_programs(1)
    slot = step % 2

    def issue(step_i, slot_i):
        base = step_i * rows_per_step

        def group(g, c):
            for u in range(GATHER_UNROLL):
                i = g * GATHER_UNROLL + u
                pltpu.make_async_copy(x_hbm.at[pl.ds(idx_ref[base + i], 1)],
                                      buf_ref.at[slot_i, pl.ds(i, 1)], sem.at[slot_i]).start()
            return c

        lax.fori_loop(0, rows_per_step // GATHER_UNROLL, group, 0)

    @pl.when(step == 0)
    def _():
        issue(step, slot)

    @pl.when(step + 1 < n_steps)
    def _():
        issue(step + 1, 1 - slot)

    pltpu.make_async_copy(x_hbm.at[pl.ds(0, rows_per_step)], buf_ref.at[slot], sem.at[slot]).wait()
    o_ref[0] = buf_ref[slot].astype(BF16)


def _gather_rows(idx_flat, x1, ne, m, rows_per_step):
    d = x1.shape[1]
    return pl.pallas_call(
        functools.partial(_gather_kernel, rows_per_step=rows_per_step),
        grid_spec=pltpu.PrefetchScalarGridSpec(
            num_scalar_prefetch=1,
            grid=(ne, m // rows_per_step),
            in_specs=[pl.BlockSpec(memory_space=pl.ANY)],
            out_specs=pl.BlockSpec((1, rows_per_step, d), lambda e, c, idx: (e, c, 0)),
            scratch_shapes=[pltpu.VMEM((2, rows_per_step, d), F32), pltpu.SemaphoreType.DMA((2,))]),
        out_shape=jax.ShapeDtypeStruct((ne, m, d), BF16),
        compiler_params=_params("arbitrary", "arbitrary"),
        name="gather_rows",
    )(idx_flat, x1)


def _ffn_hidden(x, wg, wu):
    g = jnp.dot(x, wg.astype(BF16), preferred_element_type=F32)
    u = jnp.dot(x, wu.astype(BF16), preferred_element_type=F32)
    return ((g * jax.nn.sigmoid(g)) * u).astype(BF16)


def _ffn_kernel(x_ref, gate_ref, wg_ref, wu_ref, wgt_ref, wut_ref, wd_ref, o_ref, h_ref, *,
                n_main, tf, tail, row_chunk):
    s = pl.program_id(1)
    m = x_ref.shape[1]
    row_chunks = [pl.ds(r * row_chunk, row_chunk) for r in range(m // row_chunk)]

    @pl.when(s < n_main)
    def _():
        col = pl.multiple_of(s * tf, tf)
        for rows in row_chunks:
            h_ref[rows, pl.ds(col, tf)] = _ffn_hidden(x_ref[0, rows, :], wg_ref[0], wu_ref[0])

    @pl.when(s == n_main)
    def _():
        for rows in row_chunks:
            h_ref[rows, pl.ds(n_main * tf, tail)] = _ffn_hidden(x_ref[0, rows, :], wgt_ref[0], wut_ref[0])

    @pl.when(s > n_main)
    def _():
        wd = wd_ref[0].astype(BF16)
        for rows in row_chunks:
            y = jnp.dot(h_ref[rows, :], wd, preferred_element_type=F32)
            o_ref[0, rows, :] = y * gate_ref[0, rows, :]


def _expert_ffn(xin, gate, w_gate, w_up, w_down, tf, tn):
    ne, m, d = xin.shape
    ff = w_gate.shape[2]
    n_main = ff // tf
    tail = ff - n_main * tf
    assert tail > 0 and tail % LANES == 0 and (n_main * tf) % tail == 0 and d % tn == 0
    tail_blk = (n_main * tf) // tail
    n_up = n_main + 1
    up_idx = lambda e, s: (e, 0, jnp.minimum(s, n_main - 1))
    down_idx = lambda e, s: (e, 0, jnp.maximum(s - n_up, 0))
    return pl.pallas_call(
        functools.partial(_ffn_kernel, n_main=n_main, tf=tf, tail=tail, row_chunk=min(512, m)),
        grid=(ne, n_up + d // tn),
        in_specs=[pl.BlockSpec((1, m, d), lambda e, s: (e, 0, 0)),
                  pl.BlockSpec((1, m, 1), lambda e, s: (e, 0, 0)),
                  pl.BlockSpec((1, d, tf), up_idx),
                  pl.BlockSpec((1, d, tf), up_idx),
                  pl.BlockSpec((1, d, tail), lambda e, s: (e, 0, tail_blk), pipeline_mode=pl.Buffered(1)),
                  pl.BlockSpec((1, d, tail), lambda e, s: (e, 0, tail_blk), pipeline_mode=pl.Buffered(1)),
                  pl.BlockSpec((1, ff, tn), down_idx)],
        out_specs=pl.BlockSpec((1, m, tn), down_idx),
        out_shape=jax.ShapeDtypeStruct((ne, m, d), F32),
        scratch_shapes=[pltpu.VMEM((m, ff), BF16)],
        compiler_params=_params("arbitrary", "arbitrary"),
        name="expert_ffn",
    )(xin, gate, w_gate, w_up, w_gate, w_up, w_down)


def _combine_kernel(tok_ref, y_ref, o_hbm, acc_ref, sem, *, cap):
    b = pl.program_id(0)
    e = pl.program_id(1)

    @pl.when(e == 0)
    def _():
        acc_ref[...] = jnp.zeros_like(acc_ref)

    base = (b * pl.num_programs(1) + e) * cap

    def body(s, c):
        t = tok_ref[base + s]
        acc_ref[pl.ds(t, 1), :] = acc_ref[pl.ds(t, 1), :] + y_ref[0, pl.ds(s, 1), :]
        return c

    lax.fori_loop(0, cap, body, 0)

    @pl.when(e == pl.num_programs(1) - 1)
    def _():
        copy = pltpu.make_async_copy(acc_ref, o_hbm.at[b], sem)
        copy.start()
        copy.wait()


def _combine(tok_flat, y, batch, seq, cap):
    ne, m, d = y.shape
    return pl.pallas_call(
        functools.partial(_combine_kernel, cap=cap),
        grid_spec=pltpu.PrefetchScalarGridSpec(
            num_scalar_prefetch=1,
            grid=(batch, ne),
            in_specs=[pl.BlockSpec((1, cap, d), lambda b, e, tok: (e, b, 0))],
            out_specs=pl.BlockSpec(memory_space=pl.ANY),
            scratch_shapes=[pltpu.VMEM((seq, d), F32), pltpu.SemaphoreType.DMA(())]),
        out_shape=jax.ShapeDtypeStruct((batch, seq, d), F32),
        compiler_params=_params("arbitrary", "arbitrary"),
        name="combine",
    )(tok_flat, y)


def _final_kernel(x_ref, moe_ref, g_ref, b_ref, o_ref):
    o_ref[...] = _layer_norm(DEEPNORM_ALPHA * x_ref[...] + moe_ref[...], g_ref[...], b_ref[...])


def _final_norm(x1, moe, ln_g, ln_b, tm):
    rows, d = x1.shape
    tile = pl.BlockSpec((tm, d), lambda i: (i, 0))
    row = pl.BlockSpec((1, d), lambda i: (0, 0))
    return pl.pallas_call(
        _final_kernel,
        grid=(rows // tm,),
        in_specs=[tile, tile, row, row],
        out_specs=tile,
        out_shape=jax.ShapeDtypeStruct((rows, d), F32),
        compiler_params=_params("arbitrary"),
        name="final_norm",
    )(x1, moe, ln_g, ln_b)


def _rope_tables(seq):
    pos = jnp.arange(seq, dtype=F32)
    inv_freq = ROPE_THETA ** (-jnp.arange(0, HEAD_DIM, 2, dtype=F32) / HEAD_DIM)
    ang = pos[:, None] * inv_freq[None, :]
    ang = jnp.concatenate([ang, ang, ang, ang], axis=-1)
    cos, sin = jnp.cos(ang), jnp.sin(ang)
    low = (jnp.arange(LANES) % HEAD_DIM) < HEAD_DIM // 2
    return cos, jnp.where(low, -sin, 0.0), jnp.where(low, 0.0, sin)


def kernel(x, w_in, pool_w, pool_scale, lambda_q1, lambda_k1, lambda_q2, lambda_k2, subln_g, w_out,
           ln1_g, ln1_b, w_router, w_gate, w_up, w_down, ln2_g, ln2_b):
    batch, seq, d = x.shape
    rows = batch * seq
    cap = CAPACITY_FACTOR * seq // N_EXPERTS
    cos, sin_lo, sin_hi = _rope_tables(seq)
    x2 = x.reshape(rows, d)
    for l in range(DEPTH):
        u_pool, q, k, v = _in_projection(x2, w_in[l].astype(BF16), cos, sin_lo, sin_hi, seq, tm=min(512, seq))
        pool_out = _pool_mixer(u_pool, pool_w[l], pool_scale[l][None, :], batch, seq)
        attn_out = _diff_attention(q, k, v, lambda_q1[l][None, :], lambda_k1[l][None, :], lambda_q2[l][None, :],
                                   lambda_k2[l][None, :], subln_g[l][None, :], batch, seq,
                                   tq=min(512, seq), tk=min(1024, seq))
        x1, aff_t = _out_projection(pool_out, attn_out, x2, w_out[l].astype(BF16), ln1_g[l][None, :],
                                    ln1_b[l][None, :], w_router[l].T, tm=min(512, seq))
        idx, gate = _route(aff_t, batch, seq, cap)
        idx_e = jnp.transpose(idx, (2, 0, 1))
        gate_e = jnp.transpose(gate, (2, 0, 1)).reshape(N_EXPERTS, batch * cap, 1)
        xin = _gather_rows(idx_e.reshape(-1), x1, N_EXPERTS, batch * cap, rows_per_step=min(256, cap))
        y = _expert_ffn(xin, gate_e, w_gate[l], w_up[l], w_down[l], tf=256, tn=256)
        tok_b = (jnp.transpose(idx, (0, 2, 1)) - (jnp.arange(batch) * seq)[:, None, None]).reshape(-1)
        moe = _combine(tok_b, y, batch, seq, cap)
        x2 = _final_norm(x1, moe.reshape(rows, d), ln2_g[l][None, :], ln2_b[l][None, :], tm=min(512, seq))
    return x2.reshape(batch, seq, d)
```
